```python
import jax, jax.numpy as jnp
from jax import lax
import numpy as np


D_MODEL = 1024
BATCH = 1
SEQ = 16384
DEPTH = 4
DEC_BATCH = 32
DEC_SEQ = 16
PAST_LEN = 1024

CHUNK = 64
D_MIX = D_MODEL
D_A = D_MIX // 2
D_B = D_MIX - D_A
N_GROUPS_A = 4
GROUP_DIM_A = D_A // N_GROUPS_A
GMLP_CHUNK = 128
N_HEADS_B = 8
HEAD_DIM_B = D_B // N_HEADS_B
LEFT_CHUNKS = 8
BAND = (LEFT_CHUNKS + 1) * CHUNK
MAX_REL = 128
N_REL = 2 * MAX_REL + 1
D_IN = 3 * D_A + 4 * D_B
ALPHA = (2 * DEPTH) ** 0.25
BETA = (8 * DEPTH) ** -0.25
LN_EPS = 1e-5

kernel_name = "hybrid_gmlp_bandattn_stream_step"


def _layernorm(x, g, b):
    xf = x.astype(jnp.float32)
    mu = jnp.mean(xf, axis=-1, keepdims=True)
    var = jnp.mean(jnp.square(xf - mu), axis=-1, keepdims=True)
    return ((xf - mu) * lax.rsqrt(var + LN_EPS) * g.astype(jnp.float32) + b.astype(jnp.float32)).astype(x.dtype)


def _project(x, w_in, ln_v_g, ln_v_b):
    h = jnp.einsum('bsd,de->bse', x, w_in)
    o = np.cumsum([0, D_A, D_A, D_A, D_B, D_B, D_B, D_B])
    u_a = jax.nn.gelu(h[..., o[0]:o[1]], approximate=False)
    v_a = _layernorm(jax.nn.gelu(h[..., o[1]:o[2]], approximate=False), ln_v_g, ln_v_b)
    z_a = h[..., o[2]:o[3]]
    q = h[..., o[3]:o[4]]
    k = h[..., o[4]:o[5]]
    v_b = h[..., o[5]:o[6]]
    z_b = h[..., o[6]:o[7]]
    return u_a, v_a, z_a, q, k, v_b, z_b


def _sgu_prompt(u, v, w_s, b_s):
    b, s, _ = v.shape
    vc = v.reshape(b, s // GMLP_CHUNK, GMLP_CHUNK, N_GROUPS_A, GROUP_DIM_A)
    mix = jnp.einsum('gts,bnsgc->bntgc', jnp.tril(w_s), vc) + b_s.T[None, None, :, :, None]
    return u * mix.reshape(b, s, D_A)


def _sgu_sample(u, v, w_s, b_s):
    b, n, _ = v.shape
    wm = jnp.tril(w_s)[:, :n, :n]
    vc = v.reshape(b, n, N_GROUPS_A, GROUP_DIM_A)
    mix = jnp.einsum('gts,bsgc->btgc', wm, vc) + b_s[:, :n].T[None, :, :, None]
    return u * mix.reshape(b, n, D_A)


def _rel_bias(table, n_q, n_past, n_k):
    d = jnp.arange(n_q)[:, None] + n_past - jnp.arange(n_k)[None, :]
    idx = jnp.clip(d, -MAX_REL, MAX_REL) + MAX_REL
    return table[:, idx].astype(jnp.float32)


def _band_attn_prompt(q, k, v, rel_table):
    b, s, _ = q.shape
    nc = s // CHUNK
    shp = (b, nc, CHUNK, N_HEADS_B, HEAD_DIM_B)
    qc, kc, vc = q.reshape(shp), k.reshape(shp), v.reshape(shp)
    pad = ((0, 0), (LEFT_CHUNKS, 0), (0, 0), (0, 0), (0, 0))
    kp, vp = jnp.pad(kc, pad), jnp.pad(vc, pad)
    kb = jnp.concatenate([kp[:, o:o + nc] for o in range(LEFT_CHUNKS + 1)], axis=2)
    vb = jnp.concatenate([vp[:, o:o + nc] for o in range(LEFT_CHUNKS + 1)], axis=2)
    src_chunk = jnp.arange(nc)[:, None] - LEFT_CHUNKS + jnp.arange(LEFT_CHUNKS + 1)[None, :]
    valid = jnp.repeat(src_chunk >= 0, CHUNK, axis=1)
    bias = _rel_bias(rel_table, CHUNK, BAND - CHUNK, BAND)
    scores = jnp.einsum('bnqhd,bnkhd->bnhqk', qc, kb).astype(jnp.float32) * (HEAD_DIM_B ** -0.5) + bias[None, None]
    scores = jnp.where(valid[None, :, None, None, :], scores, -jnp.inf)
    p = jax.nn.softmax(scores, axis=-1).astype(v.dtype)
    o = jnp.einsum('bnhqk,bnkhd->bnqhd', p, vb)
    return o.reshape(b, s, D_B), kc.reshape(b, s, N_HEADS_B, HEAD_DIM_B), vc.reshape(b, s, N_HEADS_B, HEAD_DIM_B)


def _band_attn_sample(q, k, v, ck, cv, rel_table):
    b, n, _ = q.shape
    n_past = ck.shape[1]
    shp = (b, n, N_HEADS_B, HEAD_DIM_B)
    qh, kh, vh = q.reshape(shp), k.reshape(shp), v.reshape(shp)
    k_all = jnp.concatenate([ck.astype(kh.dtype), kh], axis=1)
    v_all = jnp.concatenate([cv.astype(vh.dtype), vh], axis=1)
    bias = _rel_bias(rel_table, n, n_past, n_past + n)
    scores = jnp.einsum('bqhd,bkhd->bhqk', qh, k_all).astype(jnp.float32) * (HEAD_DIM_B ** -0.5) + bias[None]
    p = jax.nn.softmax(scores, axis=-1).astype(v.dtype)
    o = jnp.einsum('bhqk,bkhd->bqhd', p, v_all)
    return o.reshape(b, n, D_B), kh, vh


def _merge(x, a_out, z_a, b_out, z_b, w_out, g, bb):
    y = jnp.concatenate([a_out * jax.nn.silu(z_a), b_out * jax.nn.silu(z_b)], axis=-1)
    out = jnp.einsum('bse,ed->bsd', y, w_out)
    return _layernorm(ALPHA * x + out, g, bb)


def setup_inputs(seed: int = 0) -> dict:
    key = jax.random.key(seed)
    ks = jax.random.split(key, 13)
    kv_past = min(LEFT_CHUNKS * CHUNK, PAST_LEN)

    def nrm(k, shape):
        return jax.random.normal(k, shape, jnp.float32)

    return {
        'x_prompt': nrm(ks[0], (BATCH, SEQ, D_MODEL)),
        'x_sample': nrm(ks[1], (DEC_BATCH, DEC_SEQ, D_MODEL)),
        'cache_k': nrm(ks[2], (DEPTH, DEC_BATCH, kv_past, N_HEADS_B, HEAD_DIM_B)),
        'cache_v': nrm(ks[3], (DEPTH, DEC_BATCH, kv_past, N_HEADS_B, HEAD_DIM_B)),
        'w_in': nrm(ks[4], (DEPTH, D_MODEL, D_IN)) * (D_MODEL ** -0.5),
        'w_sgu': nrm(ks[5], (DEPTH, N_GROUPS_A, GMLP_CHUNK, GMLP_CHUNK)) * (GMLP_CHUNK ** -0.5),
        'b_sgu': 1.0 + 0.1 * nrm(ks[6], (DEPTH, N_GROUPS_A, GMLP_CHUNK)),
        'ln_v_g': 1.0 + 0.1 * nrm(ks[7], (DEPTH, D_A)),
        'ln_v_b': 0.02 * nrm(ks[8], (DEPTH, D_A)),
        'rel_bias': 0.2 * nrm(ks[9], (DEPTH, N_HEADS_B, N_REL)),
        'w_out': nrm(ks[10], (DEPTH, D_MIX, D_MODEL)) * (D_MIX ** -0.5 * BETA),
        'ln_g': 1.0 + 0.1 * nrm(ks[11], (DEPTH, D_MODEL)),
        'ln_b': 0.02 * nrm(ks[12], (DEPTH, D_MODEL)),
    }


def reference(x_prompt, x_sample, cache_k, cache_v, w_in, w_sgu, b_sgu, ln_v_g, ln_v_b, rel_bias, w_out, ln_g, ln_b):
    xp, xs = x_prompt, x_sample
    keep = min(LEFT_CHUNKS * CHUNK, xp.shape[1])
    kp_l, vp_l, ks_l, vs_l, gv_l = [], [], [], [], []
    for l in range(DEPTH):
        u_a, v_a, z_a, q, k, v_b, z_b = _project(xp, w_in[l], ln_v_g[l], ln_v_b[l])
        a_out = _sgu_prompt(u_a, v_a, w_sgu[l], b_sgu[l])
        b_out, kh, vh = _band_attn_prompt(q, k, v_b, rel_bias[l])
        kp_l.append(kh[:, -keep:])
        vp_l.append(vh[:, -keep:])
        xp = _merge(xp, a_out, z_a, b_out, z_b, w_out[l], ln_g[l], ln_b[l])
        u_a, v_a, z_a, q, k, v_b, z_b = _project(xs, w_in[l], ln_v_g[l], ln_v_b[l])
        a_out = _sgu_sample(u_a, v_a, w_sgu[l], b_sgu[l])
        b_out, kh, vh = _band_attn_sample(q, k, v_b, cache_k[l], cache_v[l], rel_bias[l])
        ks_l.append(kh)
        vs_l.append(vh)
        gv_l.append(v_a)
        xs = _merge(xs, a_out, z_a, b_out, z_b, w_out[l], ln_g[l], ln_b[l])
    k_prompt_new = jnp.stack(kp_l)
    v_prompt_new = jnp.stack(vp_l)
    k_sample_new = jnp.stack(ks_l)
    v_sample_new = jnp.stack(vs_l)
    gmlp_v_sample_new = jnp.stack(gv_l)
    return (xp, xs, k_prompt_new, v_prompt_new, k_sample_new, v_sample_new, gmlp_v_sample_new)
```

```python
import functools

import jax
import jax.numpy as jnp
from jax import lax
from jax.experimental import pallas as pl
from jax.experimental.pallas import tpu as pltpu

D_MODEL = 1024
DEPTH = 4
CHUNK = 64
D_A = 512
D_B = 512
N_GROUPS_A = 4
GROUP_DIM_A = 128
GMLP_CHUNK = 128
N_HEADS_B = 8
HEAD_DIM_B = 64
LEFT_CHUNKS = 8
PAST = LEFT_CHUNKS * CHUNK
MAX_REL = 128
ALPHA = (2 * DEPTH) ** 0.25
LN_EPS = 1e-5
SEG = 512
COL_U, COL_VA, COL_ZA, COL_Q, COL_K, COL_VB, COL_ZB = range(7)
D_IN = 7 * SEG
D_MIX = D_A + D_B

LANES = 128
HEADS_PER_TILE = LANES // HEAD_DIM_B
N_HEAD_TILES = D_B // LANES
VMEM_LIMIT_BYTES = 52 * 1024 * 1024

ROW_TILE = 512
Q_CHUNKS = 2
GQ = Q_CHUNKS * CHUNK
GK = (Q_CHUNKS + LEFT_CHUNKS) * CHUNK
BIAS_W_PROMPT = 768
SAMPLE_BLOCK = 4
BIAS_W_SAMPLE = 640

F32 = jnp.float32
BF16 = jnp.bfloat16
NEG_INF = float("-inf")


def _gelu(x):
    return 0.5 * x * (1.0 + lax.erf(x * (0.5 ** 0.5)))


def _layernorm(x, g, b):
    mu = jnp.mean(x, axis=-1, keepdims=True)
    xc = x - mu
    var = jnp.mean(xc * xc, axis=-1, keepdims=True)
    return xc * lax.rsqrt(var + LN_EPS) * g + b


def _toeplitz(r_row, n_q, n_k):
    w = r_row.shape[-1]
    x = jnp.broadcast_to(r_row, (n_q, w))
    x = pltpu.roll(x, w - n_q, 1, stride=1, stride_axis=0)
    return x[:, :n_k]


def _project(xbf, w_in_ref, col):
    return jnp.dot(xbf, w_in_ref[:, col * SEG:(col + 1) * SEG], preferred_element_type=F32)


def _softmax_pv(s_parts, v_parts):
    m = s_parts[0].max(axis=-1, keepdims=True)
    for s in s_parts[1:]:
        m = jnp.maximum(m, s.max(axis=-1, keepdims=True))
    den = None
    acc = None
    for s, v in zip(s_parts, v_parts):
        e = jnp.exp(s - m)
        d = e.sum(axis=-1, keepdims=True)
        o = jnp.dot(e.astype(BF16), v, preferred_element_type=F32)
        den = d if den is None else den + d
        acc = o if acc is None else acc + o
    return acc / den


def _split_heads_q(q):
    lane = lax.broadcasted_iota(jnp.int32, q.shape, 1) % LANES
    low = lane < HEAD_DIM_B
    return jnp.where(low, q, 0.0).astype(BF16), jnp.where(low, 0.0, q).astype(BF16)


def _head_tile_attention(q_tiles, k_parts, v_parts, bias_parts):
    outs = []
    for hh in range(HEADS_PER_TILE):
        s_parts = [
            lax.dot_general(q_tiles[hh], k, (((1,), (1,)), ((), ())), preferred_element_type=F32) + b
            for k, b in zip(k_parts, bias_parts[hh])
        ]
        outs.append(_softmax_pv(s_parts, v_parts))
    lane = lax.broadcasted_iota(jnp.int32, outs[0].shape, 1)
    return jnp.where(lane < HEAD_DIM_B, outs[0], outs[1])


def _prompt_kernel(x_ref, w_in_ref, w_out_ref, wsgu_ref, bsgu_t_ref, lnv_g_ref, lnv_b_ref,
                   rrow_ref, ln_g_ref, ln_b_ref,
                   y_ref, kout_ref, vout_ref,
                   xbf_s, q_s, k_s, v_s, u_s, va_s, sza_s, szb_s, y_s, bias_s, wtril_s, bsfull_s):
    i = pl.program_id(0)
    n_steps = pl.num_programs(0)
    tm = x_ref.shape[0]

    @pl.when(i == 0)
    def _init():
        qc = lax.broadcasted_iota(jnp.int32, (GQ, GK), 0) // CHUNK
        kc = lax.broadcasted_iota(jnp.int32, (GQ, GK), 1) // CHUNK
        in_band = (kc >= qc) & (kc <= qc + LEFT_CHUNKS)
        for h in range(N_HEADS_B):
            b = _toeplitz(rrow_ref[h], GQ, GK)
            bias_s[h] = jnp.where(in_band, b, NEG_INF)
        row = lax.broadcasted_iota(jnp.int32, (GMLP_CHUNK, GMLP_CHUNK), 0)
        col = lax.broadcasted_iota(jnp.int32, (GMLP_CHUNK, GMLP_CHUNK), 1)
        for g in range(N_GROUPS_A):
            wtril_s[g] = jnp.where(row >= col, wsgu_ref[g], 0.0).astype(BF16)
            bsfull_s[:, g * GROUP_DIM_A:(g + 1) * GROUP_DIM_A] = jnp.broadcast_to(
                bsgu_t_ref[:, g:g + 1], (GMLP_CHUNK, GROUP_DIM_A))
        k_s[0:PAST, :] = jnp.zeros((PAST, D_B), BF16)
        v_s[0:PAST, :] = jnp.zeros((PAST, D_B), BF16)

    xbf_s[...] = x_ref[...].astype(BF16)
    xbf = xbf_s[...]
    q_lo, q_hi = _split_heads_q(_project(xbf, w_in_ref, COL_Q) * (HEAD_DIM_B ** -0.5))
    q_s[0] = q_lo
    q_s[1] = q_hi
    k = _project(xbf, w_in_ref, COL_K)
    k_s[PAST:PAST + tm, :] = k.astype(BF16)
    v = _project(xbf, w_in_ref, COL_VB)
    v_s[PAST:PAST + tm, :] = v.astype(BF16)

    @pl.when(i == n_steps - 1)
    def _emit_kv():
        kout_ref[...] = k[tm - PAST:, :]
        vout_ref[...] = v[tm - PAST:, :]

    u_s[...] = _gelu(_project(xbf, w_in_ref, COL_U))
    va = _gelu(_project(xbf, w_in_ref, COL_VA))
    va_s[...] = _layernorm(va, lnv_g_ref[...], lnv_b_ref[...]).astype(BF16)
    sza_s[...] = jax.nn.silu(_project(xbf, w_in_ref, COL_ZA))
    szb_s[...] = jax.nn.silu(_project(xbf, w_in_ref, COL_ZB))

    for n in range(tm // GMLP_CHUNK):
        rs = slice(n * GMLP_CHUNK, (n + 1) * GMLP_CHUNK)
        for g in range(N_GROUPS_A):
            cs = slice(g * GROUP_DIM_A, (g + 1) * GROUP_DIM_A)
            mix = jnp.dot(wtril_s[g], va_s[rs, cs], preferred_element_type=F32) + bsfull_s[:, cs]
            y_s[rs, cs] = (u_s[rs, cs] * mix * sza_s[rs, cs]).astype(BF16)

    def attn_group(gi, first_tile):
        r = pl.multiple_of(gi * GQ, GQ)
        if first_tile:
            col = lax.broadcasted_iota(jnp.int32, (GQ, GK), 1)
            has_key = col >= PAST - gi * GQ
        for t in range(N_HEAD_TILES):
            cs = slice(t * LANES, (t + 1) * LANES)
            biases = []
            for hh in range(HEADS_PER_TILE):
                b = bias_s[t * HEADS_PER_TILE + hh]
                if first_tile:
                    b = jnp.where(has_key, b, NEG_INF)
                biases.append([b])
            q_tiles = [q_s[hh, pl.ds(r, GQ), cs] for hh in range(HEADS_PER_TILE)]
            o = _head_tile_attention(q_tiles, [k_s[pl.ds(r, GK), cs]],
                                     [v_s[pl.ds(r, GK), cs]], biases)
            y_s[pl.ds(r, GQ), D_A + t * LANES:D_A + (t + 1) * LANES] = (
                o * szb_s[pl.ds(r, GQ), cs]).astype(BF16)

    n_groups = tm // GQ

    @pl.when(i == 0)
    def _attn_first():
        lax.fori_loop(0, n_groups, lambda gi, c: (attn_group(gi, True), c)[1], 0)

    @pl.when(i > 0)
    def _attn_rest():
        lax.fori_loop(0, n_groups, lambda gi, c: (attn_group(gi, False), c)[1], 0)

    out = jnp.dot(y_s[...], w_out_ref[...], preferred_element_type=F32)
    y_ref[...] = _layernorm(ALPHA * x_ref[...] + out, ln_g_ref[...], ln_b_ref[...])

    k_s[0:PAST, :] = k_s[tm:tm + PAST, :]
    v_s[0:PAST, :] = v_s[tm:tm + PAST, :]


def _const_spec(shape):
    return pl.BlockSpec(shape, lambda i: (0,) * len(shape), pipeline_mode=pl.Buffered(1))


def _prompt_layer(x, w_in, w_out, wsgu, bsgu_t, lnv_g, lnv_b, rrow, ln_g, ln_b):
    s = x.shape[0]
    tm = ROW_TILE
    assert s % tm == 0 and tm >= PAST and tm % GQ == 0 and s >= PAST
    row_spec = pl.BlockSpec((tm, D_MODEL), lambda i: (i, 0))
    kv_spec = pl.BlockSpec((PAST, D_B), lambda i: (0, 0))
    return pl.pallas_call(
        _prompt_kernel,
        grid=(s // tm,),
        in_specs=[
            row_spec,
            _const_spec((D_MODEL, D_IN)),
            _const_spec((D_MIX, D_MODEL)),
            _const_spec((N_GROUPS_A, GMLP_CHUNK, GMLP_CHUNK)),
            _const_spec((GMLP_CHUNK, N_GROUPS_A)),
            _const_spec((1, D_A)),
            _const_spec((1, D_A)),
            _const_spec((N_HEADS_B, 1, BIAS_W_PROMPT)),
            _const_spec((1, D_MODEL)),
            _const_spec((1, D_MODEL)),
        ],
        out_specs=[row_spec, kv_spec, kv_spec],
        out_shape=[
            jax.ShapeDtypeStruct((s, D_MODEL), F32),
            jax.ShapeDtypeStruct((PAST, D_B), F32),
            jax.ShapeDtypeStruct((PAST, D_B), F32),
        ],
        scratch_shapes=[
            pltpu.VMEM((tm, D_MODEL), BF16),
            pltpu.VMEM((HEADS_PER_TILE, tm, D_B), BF16),
            pltpu.VMEM((PAST + tm, D_B), BF16),
            pltpu.VMEM((PAST + tm, D_B), BF16),
            pltpu.VMEM((tm, D_A), F32),
            pltpu.VMEM((tm, D_A), BF16),
            pltpu.VMEM((tm, D_A), F32),
            pltpu.VMEM((tm, D_B), F32),
            pltpu.VMEM((tm, D_MIX), BF16),
            pltpu.VMEM((N_HEADS_B, GQ, GK), F32),
            pltpu.VMEM((N_GROUPS_A, GMLP_CHUNK, GMLP_CHUNK), BF16),
            pltpu.VMEM((GMLP_CHUNK, D_A), F32),
        ],
        compiler_params=pltpu.CompilerParams(
            dimension_semantics=("arbitrary",), vmem_limit_bytes=VMEM_LIMIT_BYTES),
        name="prompt_layer",
    )(x, w_in, w_out, wsgu, bsgu_t, lnv_g, lnv_b, rrow, ln_g, ln_b)


def _sample_kernel(x_ref, w_in_ref, w_out_ref, wbd_ref, bsgu_t_ref, lnv_g_ref, lnv_b_ref,
                   rrow_ref, ln_g_ref, ln_b_ref, ck_ref, cv_ref,
                   y_ref, kout_ref, vout_ref, gv_ref,
                   q_s, k_s, v_s, szb_s, y_s, bias_s, *, n_new):
    i = pl.program_id(0)
    n_steps = pl.num_programs(0)
    rows = x_ref.shape[0]
    n_past = ck_ref.shape[1]

    @pl.when(i == 0)
    def _project_all():
        for h in range(N_HEADS_B):
            bias_s[h] = _toeplitz(rrow_ref[h], n_new, BIAS_W_SAMPLE)
        xbf = x_ref[...].astype(BF16)
        q_lo, q_hi = _split_heads_q(_project(xbf, w_in_ref, COL_Q) * (HEAD_DIM_B ** -0.5))
        q_s[0] = q_lo
        q_s[1] = q_hi
        k = _project(xbf, w_in_ref, COL_K)
        kout_ref[...] = k
        k_s[...] = k.astype(BF16)
        v = _project(xbf, w_in_ref, COL_VB)
        vout_ref[...] = v
        v_s[...] = v.astype(BF16)
        u = _gelu(_project(xbf, w_in_ref, COL_U))
        va = _gelu(_project(xbf, w_in_ref, COL_VA))
        va = _layernorm(va, lnv_g_ref[...], lnv_b_ref[...])
        gv_ref[...] = va
        sza = jax.nn.silu(_project(xbf, w_in_ref, COL_ZA))
        szb_s[...] = jax.nn.silu(_project(xbf, w_in_ref, COL_ZB))
        vabf = va.astype(BF16)
        for n in range(rows // GMLP_CHUNK):
            rs = slice(n * GMLP_CHUNK, (n + 1) * GMLP_CHUNK)
            for g in range(N_GROUPS_A):
                cs = slice(g * GROUP_DIM_A, (g + 1) * GROUP_DIM_A)
                mix = jnp.dot(wbd_ref[g].astype(BF16), vabf[rs, cs], preferred_element_type=F32)
                mix = mix + jnp.broadcast_to(bsgu_t_ref[:, g:g + 1], (GMLP_CHUNK, GROUP_DIM_A))
                y_s[rs, cs] = (u[rs, cs] * mix * sza[rs, cs]).astype(BF16)

    for bb in range(SAMPLE_BLOCK):
        r = pl.multiple_of((i * SAMPLE_BLOCK + bb) * n_new, n_new)
        for t in range(N_HEAD_TILES):
            cs = slice(t * LANES, (t + 1) * LANES)
            biases = [
                [bias_s[t * HEADS_PER_TILE + hh][:, :n_past],
                 bias_s[t * HEADS_PER_TILE + hh][:, n_past:n_past + n_new]]
                for hh in range(HEADS_PER_TILE)
            ]
            o = _head_tile_attention(
                [q_s[hh, pl.ds(r, n_new), cs] for hh in range(HEADS_PER_TILE)],
                [ck_ref[bb, :, cs].astype(BF16), k_s[pl.ds(r, n_new), cs]],
                [cv_ref[bb, :, cs].astype(BF16), v_s[pl.ds(r, n_new), cs]],
                biases)
            y_s[pl.ds(r, n_new), D_A + t * LANES:D_A + (t + 1) * LANES] = (
                o * szb_s[pl.ds(r, n_new), cs]).astype(BF16)

    @pl.when(i == n_steps - 1)
    def _merge():
        out = jnp.dot(y_s[...], w_out_ref[...], preferred_element_type=F32)
        y_ref[...] = _layernorm(ALPHA * x_ref[...] + out, ln_g_ref[...], ln_b_ref[...])


def _sample_layer(x, w_in, w_out, wbd, bsgu_t, lnv_g, lnv_b, rrow, ln_g, ln_b, ck, cv, n_new):
    rows = x.shape[0]
    n_streams, n_past, _ = ck.shape
    assert rows == n_streams * n_new and n_streams % SAMPLE_BLOCK == 0
    assert rows % GMLP_CHUNK == 0 and GMLP_CHUNK % n_new == 0 and n_past == PAST
    assert n_past + 2 * n_new <= BIAS_W_SAMPLE
    full = lambda shape: pl.BlockSpec(shape, lambda i: (0,) * len(shape))
    cache_spec = pl.BlockSpec((SAMPLE_BLOCK, n_past, D_B), lambda i: (i, 0, 0))
    return pl.pallas_call(
        functools.partial(_sample_kernel, n_new=n_new),
        grid=(n_streams // SAMPLE_BLOCK,),
        in_specs=[
            full((rows, D_MODEL)),
            _const_spec((D_MODEL, D_IN)),
            _const_spec((D_MIX, D_MODEL)),
            _const_spec((N_GROUPS_A, GMLP_CHUNK, GMLP_CHUNK)),
            _const_spec((GMLP_CHUNK, N_GROUPS_A)),
            _const_spec((1, D_A)),
            _const_spec((1, D_A)),
            _const_spec((N_HEADS_B, 1, BIAS_W_SAMPLE)),
            _const_spec((1, D_MODEL)),
            _const_spec((1, D_MODEL)),
            cache_spec,
            cache_spec,
        ],
        out_specs=[full((rows, D_MODEL)), full((rows, D_B)), full((rows, D_B)), full((rows, D_A))],
        out_shape=[
            jax.ShapeDtypeStruct((rows, D_MODEL), F32),
            jax.ShapeDtypeStruct((rows, D_B), F32),
            jax.ShapeDtypeStruct((rows, D_B), F32),
            jax.ShapeDtypeStruct((rows, D_A), F32),
        ],
        scratch_shapes=[
            pltpu.VMEM((HEADS_PER_TILE, rows, D_B), BF16),
            pltpu.VMEM((rows, D_B), BF16),
            pltpu.VMEM((rows, D_B), BF16),
            pltpu.VMEM((rows, D_B), F32),
            pltpu.VMEM((rows, D_MIX), BF16),
            pltpu.VMEM((N_HEADS_B, n_new, BIAS_W_SAMPLE), F32),
        ],
        compiler_params=pltpu.CompilerParams(
            dimension_semantics=("arbitrary",), vmem_limit_bytes=VMEM_LIMIT_BYTES),
        name="sample_layer",
    )(x, w_in, w_out, wbd, bsgu_t, lnv_g, lnv_b, rrow, ln_g, ln_b, ck, cv)


def _bias_rows(rel_bias, n_q, width):
    offset = jnp.arange(width) - n_q
    idx = jnp.clip(PAST - offset, -MAX_REL, MAX_REL) + MAX_REL
    return rel_bias[:, :, idx][:, :, None, :]


def kernel(x_prompt, x_sample, cache_k, cache_v, w_in, w_sgu, b_sgu, ln_v_g, ln_v_b, rel_bias,
           w_out, ln_g, ln_b):
    batch, seq, _ = x_prompt.shape
    n_streams, n_new, _ = x_sample.shape
    n_past = cache_k.shape[2]
    assert batch == 1
    keep = min(PAST, seq)

    w_in_bf = w_in.astype(BF16)
    w_out_bf = w_out.astype(BF16)
    bsgu_t = jnp.swapaxes(b_sgu, 1, 2)
    rrow_p = _bias_rows(rel_bias, GQ, BIAS_W_PROMPT)
    rrow_s = _bias_rows(rel_bias, n_new, BIAS_W_SAMPLE)
    reps = GMLP_CHUNK // n_new
    blk = jnp.tril(w_sgu)[:, :, :n_new, :n_new]
    same = (jnp.arange(GMLP_CHUNK)[:, None] // n_new) == (jnp.arange(GMLP_CHUNK)[None, :] // n_new)
    wbd = jnp.where(same, jnp.tile(blk, (1, 1, reps, reps)), 0.0)
    bsgu_t_s = jnp.tile(bsgu_t[:, :n_new, :], (1, reps, 1))
    ck = cache_k.reshape(DEPTH, n_streams, n_past, D_B)
    cv = cache_v.reshape(DEPTH, n_streams, n_past, D_B)

    xp = x_prompt.reshape(seq, D_MODEL)
    xs = x_sample.reshape(n_streams * n_new, D_MODEL)
    kp_l, vp_l, ks_l, vs_l, gv_l = [], [], [], [], []
    for l in range(DEPTH):
        lnv_g, lnv_b = ln_v_g[l][None, :], ln_v_b[l][None, :]
        g, b = ln_g[l][None, :], ln_b[l][None, :]
        xp, kp, vp = _prompt_layer(xp, w_in_bf[l], w_out_bf[l], w_sgu[l], bsgu_t[l], lnv_g, lnv_b,
                                   rrow_p[l], g, b)
        xs, ks, vs, gv = _sample_layer(xs, w_in_bf[l], w_out_bf[l], wbd[l], bsgu_t_s[l], lnv_g,
                                       lnv_b, rrow_s[l], g, b, ck[l], cv[l], n_new)
        kp_l.append(kp[-keep:].reshape(batch, keep, N_HEADS_B, HEAD_DIM_B))
        vp_l.append(vp[-keep:].reshape(batch, keep, N_HEADS_B, HEAD_DIM_B))
        ks_l.append(ks.reshape(n_streams, n_new, N_HEADS_B, HEAD_DIM_B))
        vs_l.append(vs.reshape(n_streams, n_new, N_HEADS_B, HEAD_DIM_B))
        gv_l.append(gv.reshape(n_streams, n_new, D_A))
    return (xp.reshape(batch, seq, D_MODEL), xs.reshape(n_streams, n_new, D_MODEL),
            jnp.stack(kp_l), jnp.stack(vp_l), jnp.stack(ks_l), jnp.stack(vs_l), jnp.stack(gv_l))
```
